```python
import math
import jax
import jax.numpy as jnp
from jax import lax
import numpy as np

D_MODEL = 2048
BATCH = 2
SEQ = 4096
DEPTH = 1
DEC_BATCH = 128
DEC_SEQ = 4
PAST_LEN = 2048
PAGE_SIZE = 128

D_MIX = D_MODEL
D_ATT = D_MIX // 2
D_SSM = D_MIX // 4
D_MEM = D_MIX - D_ATT - D_SSM
HEAD_DIM = 128
N_HEADS = D_ATT // HEAD_DIM
N_KV_HEADS = 2
GQA = N_HEADS // N_KV_HEADS
N_IDX_HEADS = 16
IDX_DIM = 64
MAX_TOPK = 256
Q_BLOCK = 128
SSM_GROUP = 16
N_SSM_GROUPS = D_SSM // SSM_GROUP
SSM_STATE = 64
N_MEM = 256
N_MEM_HEADS = 4
MEM_HEAD_DIM = D_MEM // N_MEM_HEADS
EPS = 1e-6
SPLITS = (D_ATT, N_KV_HEADS * HEAD_DIM, N_KV_HEADS * HEAD_DIM, D_ATT, N_IDX_HEADS * IDX_DIM, IDX_DIM, N_IDX_HEADS, D_SSM, D_SSM, D_MEM, D_MEM)
D_IN = D_ATT * 2 + N_KV_HEADS * HEAD_DIM * 2 + N_IDX_HEADS * IDX_DIM + IDX_DIM + N_IDX_HEADS + D_SSM * 2 + D_MEM * 2

kernel_name = 'hymba_dsa_s5_memory_decoder_step'


def rmsnorm(x, g):
    xf = x.astype(jnp.float32)
    y = xf * lax.rsqrt(jnp.mean(xf * xf, axis=-1, keepdims=True) + EPS)
    return (y * g.astype(jnp.float32)).astype(x.dtype)


def alibi_slopes():
    return jnp.power(2.0, -8.0 * jnp.arange(1, N_HEADS + 1, dtype=jnp.float32) / N_HEADS)


def project(x, g_norm, w_in):
    h = rmsnorm(x, g_norm)
    z = h @ w_in
    offs = [int(o) for o in np.cumsum(SPLITS)[:-1]]
    return jnp.split(z, offs, axis=-1)


def dsa_attend(q, qi, wi, q_pos, k, v, ki, k_pos, topk):
    f32 = jnp.float32
    rel = jax.nn.relu(jnp.einsum('bthd,bsd->bths', qi.astype(f32), ki.astype(f32)) * (IDX_DIM ** -0.5))
    score = jnp.einsum('bth,bths->bts', wi.astype(f32) * (N_IDX_HEADS ** -0.5), rel)
    causal = k_pos[None, :] <= q_pos[:, None]
    score = jnp.where(causal[None], score, -jnp.inf)
    _, idx = lax.top_k(score, topk)
    sel_pos = k_pos[idx]
    valid = sel_pos <= q_pos[None, :, None]
    gather = jax.vmap(lambda rows, ii: rows[ii])
    k_sel = gather(k, idx).astype(f32)
    v_sel = gather(v, idx).astype(f32)
    b, t = q.shape[0], q.shape[1]
    qg = q.reshape(b, t, N_KV_HEADS, GQA, HEAD_DIM).astype(f32)
    s = jnp.einsum('btkgd,btskd->btkgs', qg, k_sel) * (HEAD_DIM ** -0.5)
    dist = (q_pos[None, :, None] - sel_pos).astype(f32)
    slopes = alibi_slopes().reshape(N_KV_HEADS, GQA)
    s = s - slopes[None, None, :, :, None] * dist[:, :, None, None, :]
    s = jnp.where(valid[:, :, None, None, :], s, -jnp.inf)
    p = jax.nn.softmax(s, axis=-1)
    o = jnp.einsum('btkgs,btskd->btkgd', p, v_sel)
    return o.reshape(b, t, N_HEADS * HEAD_DIM).astype(q.dtype)


def dsa_prompt(q, qi, wi, k, v, ki):
    b, L = q.shape[0], q.shape[1]
    nb = L // Q_BLOCK
    pos = jnp.arange(L, dtype=jnp.int32)
    topk = min(MAX_TOPK, L // 4)

    def blocks(a):
        return jnp.moveaxis(a.reshape((b, nb, Q_BLOCK) + a.shape[2:]), 1, 0)

    def step(args):
        qb, qib, wib, pb = args
        return dsa_attend(qb, qib, wib, pb, k, v, ki, pos, topk)

    o = lax.map(step, (blocks(q), blocks(qi), blocks(wi), pos.reshape(nb, Q_BLOCK)))
    return jnp.moveaxis(o, 0, 1).reshape(b, L, D_ATT)


def _complex_linear_combine(e1, e2):
    a1r, a1i, b1r, b1i = e1
    a2r, a2i, b2r, b2i = e2
    return (a1r * a2r - a1i * a2i, a1r * a2i + a1i * a2r,
            a2r * b1r - a2i * b1i + b2r, a2r * b1i + a2i * b1r + b2i)


def s5_branch(u, h0_re, h0_im, lam_re, lam_im, log_dt, b_re, b_im, c_re, c_im, d_skip, w_glu):
    f32 = jnp.float32
    b, t = u.shape[0], u.shape[1]
    uf = u.astype(f32).reshape(b, t, N_SSM_GROUPS, SSM_GROUP)
    lr, li = lam_re.astype(f32), lam_im.astype(f32)
    dt = jnp.exp(log_dt.astype(f32))[:, None]
    mag = jnp.exp(lr * dt)
    ar, ai = mag * jnp.cos(li * dt), mag * jnp.sin(li * dt)
    den = lr * lr + li * li
    cr = (lr * (ar - 1.0) + li * ai) / den
    ci = (lr * ai - li * (ar - 1.0)) / den
    br, bi = b_re.astype(f32), b_im.astype(f32)
    bbr = cr[..., None] * br - ci[..., None] * bi
    bbi = cr[..., None] * bi + ci[..., None] * br
    xr = jnp.einsum('btgc,gnc->btgn', uf, bbr)
    xi = jnp.einsum('btgc,gnc->btgn', uf, bbi)
    h0r, h0i = h0_re.astype(f32), h0_im.astype(f32)
    xr = xr.at[:, 0].add(ar * h0r - ai * h0i)
    xi = xi.at[:, 0].add(ar * h0i + ai * h0r)
    a_r = jnp.broadcast_to(ar, xr.shape)
    a_i = jnp.broadcast_to(ai, xr.shape)
    _, _, hr, hi = lax.associative_scan(_complex_linear_combine, (a_r, a_i, xr, xi), axis=1)
    y = (jnp.einsum('btgn,gcn->btgc', hr, c_re.astype(f32))
         - jnp.einsum('btgn,gcn->btgc', hi, c_im.astype(f32))
         + d_skip.astype(f32) * uf)
    y = jax.nn.gelu(y.reshape(b, t, D_SSM))
    a, g = jnp.split(y @ w_glu.astype(f32), 2, axis=-1)
    out = a * jax.nn.sigmoid(g)
    return out.astype(u.dtype), hr[:, -1].astype(h0_re.dtype), hi[:, -1].astype(h0_im.dtype)


def mem_kv(mem, g_mem, w_mem_kv):
    b = mem.shape[0]
    kv = rmsnorm(mem, g_mem) @ w_mem_kv
    mk, mv = jnp.split(kv, 2, axis=-1)
    return (mk.reshape(b, N_MEM, N_MEM_HEADS, MEM_HEAD_DIM),
            mv.reshape(b, N_MEM, N_MEM_HEADS, MEM_HEAD_DIM))


def mem_attend(qm, mk, mv):
    f32 = jnp.float32
    b, t = qm.shape[0], qm.shape[1]
    q = qm.reshape(b, t, N_MEM_HEADS, MEM_HEAD_DIM).astype(f32)
    s = jnp.einsum('bthd,bmhd->bhtm', q, mk.astype(f32)) * (MEM_HEAD_DIM ** -0.5)
    p = jax.nn.softmax(s, axis=-1)
    o = jnp.einsum('bhtm,bmhd->bthd', p, mv.astype(f32))
    return o.reshape(b, t, D_MEM).astype(qm.dtype)


def gather_pages(pool, page_table):
    pages = pool[page_table]
    db, n_pages = page_table.shape
    return pages.reshape((db, n_pages * pool.shape[1]) + pool.shape[2:])


def mixer_layer(x, past, h0_re, h0_im, mem_k, mem_v, g_norm, w_in, lam_re, lam_im, log_dt,
                b_re, b_im, c_re, c_im, d_skip, w_glu, w_out):
    b, t = x.shape[0], x.shape[1]
    q, k, v, gate_a, qi, ki, wi, u, gate_s, qm, gate_m = project(x, g_norm, w_in)
    q = q.reshape(b, t, N_HEADS, HEAD_DIM)
    k = k.reshape(b, t, N_KV_HEADS, HEAD_DIM)
    v = v.reshape(b, t, N_KV_HEADS, HEAD_DIM)
    qi = qi.reshape(b, t, N_IDX_HEADS, IDX_DIM)
    if past is None:
        o_a = dsa_prompt(q, qi, wi, k, v, ki)
    else:
        k_past, v_past, ki_past = past
        l_past = k_past.shape[1]
        k_all = jnp.concatenate([k_past, k.astype(k_past.dtype)], axis=1)
        v_all = jnp.concatenate([v_past, v.astype(v_past.dtype)], axis=1)
        ki_all = jnp.concatenate([ki_past, ki.astype(ki_past.dtype)], axis=1)
        q_pos = l_past + jnp.arange(t, dtype=jnp.int32)
        k_pos = jnp.arange(l_past + t, dtype=jnp.int32)
        topk = min(MAX_TOPK, (l_past + t) // 4)
        o_a = dsa_attend(q, qi, wi, q_pos, k_all, v_all, ki_all, k_pos, topk)
    o_s, h_re, h_im = s5_branch(u, h0_re, h0_im, lam_re, lam_im, log_dt, b_re, b_im, c_re, c_im, d_skip, w_glu)
    o_m = mem_attend(qm, mem_k, mem_v)
    mix = jnp.concatenate([o_a * jax.nn.silu(gate_a), o_s * jax.nn.silu(gate_s), o_m * jax.nn.silu(gate_m)], axis=-1)
    y = x + mix @ w_out
    return y, k, v, ki, h_re, h_im


def setup_inputs(seed: int = 0) -> dict:
    key = jax.random.key(seed)
    ks = jax.random.split(key, 32)
    f32 = jnp.float32
    n_pages = PAST_LEN // PAGE_SIZE
    n_used = DEC_BATCH * n_pages
    n_pool = n_used + max(1, n_used // 4)

    def nrm(k, shape, s=1.0):
        return s * jax.random.normal(k, shape, f32)

    page_table = jax.random.permutation(ks[0], n_pool)[:n_used].reshape(DEC_BATCH, n_pages).astype(jnp.int32)
    lam_im0 = math.pi * jnp.arange(SSM_STATE, dtype=f32)
    return {
        'x_prompt': nrm(ks[1], (BATCH, SEQ, D_MODEL)),
        'x_sample': nrm(ks[2], (DEC_BATCH, DEC_SEQ, D_MODEL)),
        'mem_prompt': nrm(ks[3], (BATCH, N_MEM, D_MODEL)),
        'cache_k': nrm(ks[4], (DEPTH, n_pool, PAGE_SIZE, N_KV_HEADS, HEAD_DIM)),
        'cache_v': nrm(ks[5], (DEPTH, n_pool, PAGE_SIZE, N_KV_HEADS, HEAD_DIM)),
        'cache_kidx': nrm(ks[6], (DEPTH, n_pool, PAGE_SIZE, IDX_DIM)),
        'cache_mem_k': nrm(ks[7], (DEPTH, DEC_BATCH, N_MEM, N_MEM_HEADS, MEM_HEAD_DIM)),
        'cache_mem_v': nrm(ks[8], (DEPTH, DEC_BATCH, N_MEM, N_MEM_HEADS, MEM_HEAD_DIM)),
        'state_ssm_re': nrm(ks[9], (DEPTH, DEC_BATCH, N_SSM_GROUPS, SSM_STATE), 0.3),
        'state_ssm_im': nrm(ks[10], (DEPTH, DEC_BATCH, N_SSM_GROUPS, SSM_STATE), 0.3),
        'page_table': page_table,
        'g_norm': 1.0 + nrm(ks[11], (DEPTH, D_MODEL), 0.02),
        'w_in': nrm(ks[12], (DEPTH, D_MODEL, D_IN), D_MODEL ** -0.5),
        'g_mem': 1.0 + nrm(ks[13], (DEPTH, D_MODEL), 0.02),
        'w_mem_kv': nrm(ks[14], (DEPTH, D_MODEL, 2 * D_MEM), D_MODEL ** -0.5),
        'lam_re': -0.5 + nrm(ks[15], (DEPTH, N_SSM_GROUPS, SSM_STATE), 0.01),
        'lam_im': lam_im0 + nrm(ks[16], (DEPTH, N_SSM_GROUPS, SSM_STATE), 0.01),
        'log_dt': jax.random.uniform(ks[17], (DEPTH, N_SSM_GROUPS), f32, math.log(1e-3), math.log(1e-1)),
        'b_re': nrm(ks[18], (DEPTH, N_SSM_GROUPS, SSM_STATE, SSM_GROUP), (2 * SSM_GROUP) ** -0.5),
        'b_im': nrm(ks[19], (DEPTH, N_SSM_GROUPS, SSM_STATE, SSM_GROUP), (2 * SSM_GROUP) ** -0.5),
        'c_re': nrm(ks[20], (DEPTH, N_SSM_GROUPS, SSM_GROUP, SSM_STATE), SSM_STATE ** -0.5),
        'c_im': nrm(ks[21], (DEPTH, N_SSM_GROUPS, SSM_GROUP, SSM_STATE), SSM_STATE ** -0.5),
        'd_skip': nrm(ks[22], (DEPTH, N_SSM_GROUPS, SSM_GROUP)),
        'w_glu': nrm(ks[23], (DEPTH, D_SSM, 2 * D_SSM), D_SSM ** -0.5),
        'w_out': nrm(ks[24], (DEPTH, D_MIX, D_MODEL), D_MIX ** -0.5),
        'g_final': 1.0 + nrm(ks[25], (D_MODEL,), 0.02),
    }


def reference(x_prompt, x_sample, mem_prompt, cache_k, cache_v, cache_kidx, cache_mem_k, cache_mem_v,
              state_ssm_re, state_ssm_im, page_table, g_norm, w_in, g_mem, w_mem_kv, lam_re, lam_im,
              log_dt, b_re, b_im, c_re, c_im, d_skip, w_glu, w_out, g_final):
    hp, hs = x_prompt, x_sample
    zeros_h = jnp.zeros((x_prompt.shape[0], N_SSM_GROUPS, SSM_STATE), state_ssm_re.dtype)
    kp_l, vp_l, kip_l, mkp_l, mvp_l, hrp_l, hip_l = [], [], [], [], [], [], []
    ks_l, vs_l, kis_l, hrs_l, his_l = [], [], [], [], []
    for l in range(DEPTH):
        ssm_w = (lam_re[l], lam_im[l], log_dt[l], b_re[l], b_im[l], c_re[l], c_im[l], d_skip[l], w_glu[l], w_out[l])
        mk_p, mv_p = mem_kv(mem_prompt, g_mem[l], w_mem_kv[l])
        hp, kp, vp, kip, hrp, hip = mixer_layer(hp, None, zeros_h, zeros_h, mk_p, mv_p, g_norm[l], w_in[l], *ssm_w)
        past = (gather_pages(cache_k[l], page_table), gather_pages(cache_v[l], page_table),
                gather_pages(cache_kidx[l], page_table))
        hs, ksn, vsn, kisn, hrs, his = mixer_layer(hs, past, state_ssm_re[l], state_ssm_im[l], cache_mem_k[l],
                                                 cache_mem_v[l], g_norm[l], w_in[l], *ssm_w)
        kp_l.append(kp); vp_l.append(vp); kip_l.append(kip); mkp_l.append(mk_p); mvp_l.append(mv_p)
        hrp_l.append(hrp); hip_l.append(hip)
        ks_l.append(ksn); vs_l.append(vsn); kis_l.append(kisn); hrs_l.append(hrs); his_l.append(his)
    y_prompt = rmsnorm(hp, g_final)
    y_sample = rmsnorm(hs, g_final)
    new_k_prompt = jnp.stack(kp_l)
    new_v_prompt = jnp.stack(vp_l)
    new_kidx_prompt = jnp.stack(kip_l)
    new_mem_k_prompt = jnp.stack(mkp_l)
    new_mem_v_prompt = jnp.stack(mvp_l)
    new_ssm_re_prompt = jnp.stack(hrp_l)
    new_ssm_im_prompt = jnp.stack(hip_l)
    new_k_sample = jnp.stack(ks_l)
    new_v_sample = jnp.stack(vs_l)
    new_kidx_sample = jnp.stack(kis_l)
    new_ssm_re_sample = jnp.stack(hrs_l)
    new_ssm_im_sample = jnp.stack(his_l)
    return (y_prompt, y_sample, new_k_prompt, new_v_prompt, new_kidx_prompt, new_mem_k_prompt, new_mem_v_prompt,
            new_ssm_re_prompt, new_ssm_im_prompt, new_k_sample, new_v_sample, new_kidx_sample,
            new_ssm_re_sample, new_ssm_im_sample)
```

```python
import functools

import jax
import jax.numpy as jnp
from jax import lax
from jax.experimental import pallas as pl
from jax.experimental.pallas import tpu as pltpu

F32 = jnp.float32
BF16 = jnp.bfloat16
I32 = jnp.int32

D_MODEL = 2048
SEQ = 4096
DEC_SEQ = 4
PAST_LEN = 2048
PAGE = 128
N_PAGES = PAST_LEN // PAGE
D_ATT = 1024
D_SSM = 512
D_MEM = 512
HEAD_DIM = 128
N_HEADS = 8
N_KV = 2
GQA = N_HEADS // N_KV
N_IDX = 16
IDX_DIM = 64
TOPK = 256
SSM_GROUP = 16
N_GROUPS = 32
SSM_STATE = 64
N_STATE = N_GROUPS * SSM_STATE
N_MEM = 256
N_MEM_HEADS = 4
EPS = 1e-6

LANES = 128
SUBLANES = 8
VMEM_LIMIT = 56 * 1024 * 1024

C_Q, C_QI, C_GA, C_K, C_V, C_U, C_GS, C_QM, C_GM, C_KW = 0, 1024, 2048, 3072, 3328, 3584, 4096, 4608, 5120, 5632
D_INP = 5760
W_IDX_SCALE = (N_IDX ** -0.5) * (IDX_DIM ** -0.5)
ATT_SCALE = HEAD_DIM ** -0.5
NEG = -1e30
INT_MIN = -2 ** 31
KEY_OF_NEG_INF = -2139095041
NT = (((1,), (1,)), ((), ()))


def _cparams(sem):
    return pltpu.CompilerParams(dimension_semantics=sem, vmem_limit_bytes=VMEM_LIMIT)


def _sort_key(s):
    b = lax.bitcast_convert_type(s, I32)
    return b ^ ((b >> 31) & 0x7FFFFFFF)


def _sigmoid(x):
    return 1.0 / (1.0 + jnp.exp(-x))


def _proj_kernel(x_ref, g_ref, w_ref, o_ref, ob_ref, h_ref):
    @pl.when(pl.program_id(1) == 0)
    def _():
        x = x_ref[...]
        ms = jnp.mean(x * x, axis=-1, keepdims=True)
        h_ref[...] = (x * lax.rsqrt(ms + EPS) * g_ref[...]).astype(BF16)

    z = jnp.dot(h_ref[...], w_ref[...], preferred_element_type=F32)
    o_ref[...] = z
    ob_ref[...] = z.astype(BF16)


def _proj(x, g, w_bf16, tm, tn):
    m, d = x.shape
    n = w_bf16.shape[1]
    return pl.pallas_call(
        _proj_kernel,
        grid=(m // tm, n // tn),
        in_specs=[pl.BlockSpec((tm, d), lambda i, j: (i, 0)),
                  pl.BlockSpec((1, d), lambda i, j: (0, 0)),
                  pl.BlockSpec((d, tn), lambda i, j: (0, j))],
        out_specs=[pl.BlockSpec((tm, tn), lambda i, j: (i, j)),
                   pl.BlockSpec((tm, tn), lambda i, j: (i, j))],
        out_shape=[jax.ShapeDtypeStruct((m, n), F32), jax.ShapeDtypeStruct((m, n), BF16)],
        scratch_shapes=[pltpu.VMEM((tm, d), BF16)],
        compiler_params=_cparams(("arbitrary", "arbitrary")),
        name="proj",
    )(x, g.reshape(1, d), w_bf16)


TQ = 256
NQT = SEQ // TQ


def _dsa_prompt_kernel(q_ref, qi_ref, k_ref, v_ref, kwk_ref, kwq_ref, o_ref,
                       qis_ref, wib_ref, sc_ref, m_ref, l_ref, acc_ref):
    i = pl.program_id(1)
    row = lax.broadcasted_iota(I32, (TQ, TQ), 0)
    col = lax.broadcasted_iota(I32, (TQ, TQ), 1)

    for h in range(N_IDX):
        qis_ref[h] = qi_ref[:, h * IDX_DIM:(h + 1) * IDX_DIM]
        w = kwq_ref[:, IDX_DIM + h:IDX_DIM + h + 1] * W_IDX_SCALE
        wib_ref[h] = jnp.broadcast_to(w, (TQ, LANES))

    def score_tile(kt):
        kk = kwk_ref[pl.ds(pl.multiple_of(kt * TQ, TQ), TQ), :][:, :IDX_DIM]
        acc = jnp.zeros((TQ, TQ), F32)
        for h in range(N_IDX):
            z = lax.dot_general(qis_ref[h], kk, NT, preferred_element_type=F32)
            acc = acc + jnp.tile(wib_ref[h], (1, TQ // LANES)) * jnp.maximum(z, 0.0)
        return acc

    def fill(kt, carry):
        sc_ref[kt] = _sort_key(score_tile(kt))
        return carry

    lax.fori_loop(0, i, fill, 0)
    sc_ref[i] = _sort_key(jnp.where(col <= row, score_tile(i), -jnp.inf))

    def bit_body(bi, tu):
        cand_u = tu | jnp.left_shift(jnp.int32(1), 31 - bi)
        cand_s = jnp.tile(cand_u ^ INT_MIN, (1, TQ // LANES))

        def count(kt, acc):
            hit = jnp.where(sc_ref[kt] >= cand_s, 1.0, 0.0)
            return acc + hit[:, :LANES] + hit[:, LANES:]

        acc = lax.fori_loop(0, i + 1, count, jnp.zeros((TQ, LANES), F32))
        cnt = jnp.sum(acc, axis=1, keepdims=True)
        return jnp.where(cnt >= TOPK, cand_u, tu)

    tu = lax.fori_loop(0, 32, bit_body, jnp.zeros((TQ, LANES), I32))
    thr = jnp.tile(jnp.maximum(tu ^ INT_MIN, KEY_OF_NEG_INF + 1), (1, TQ // LANES))

    m_ref[...] = jnp.full(m_ref.shape, NEG, F32)
    l_ref[...] = jnp.zeros(l_ref.shape, F32)
    acc_ref[...] = jnp.zeros(acc_ref.shape, F32)

    def attend(kt, carry):
        mask = sc_ref[kt] >= thr
        dist = ((i - kt) * TQ + row - col).astype(F32)
        k0 = pl.multiple_of(kt * TQ, TQ)
        kb = k_ref[pl.ds(k0, TQ), :]
        vb = v_ref[pl.ds(k0, TQ), :]
        for h in range(N_HEADS):
            g = h // GQA
            s = lax.dot_general(q_ref[:, h * HEAD_DIM:(h + 1) * HEAD_DIM], kb[:, g * HEAD_DIM:(g + 1) * HEAD_DIM],
                                NT, preferred_element_type=F32)
            s = s * ATT_SCALE - (2.0 ** -(h + 1)) * dist
            s = jnp.where(mask, s, NEG)
            m_old = m_ref[h]
            m_new = jnp.maximum(m_old, jnp.max(s, axis=1, keepdims=True))
            alpha = jnp.exp(m_old - m_new)
            p = jnp.exp(s - jnp.tile(m_new, (1, TQ // LANES)))
            l_ref[h] = alpha * l_ref[h] + jnp.sum(p, axis=1, keepdims=True)
            acc_ref[h] = alpha * acc_ref[h] + jnp.dot(p.astype(BF16), vb[:, g * HEAD_DIM:(g + 1) * HEAD_DIM],
                                                      preferred_element_type=F32)
            m_ref[h] = m_new
        return carry

    lax.fori_loop(0, i + 1, attend, 0)
    for h in range(N_HEADS):
        o_ref[:, h * HEAD_DIM:(h + 1) * HEAD_DIM] = acc_ref[h] / l_ref[h]


def _dsa_prompt(z, zb, batch):
    t = batch * SEQ
    return pl.pallas_call(
        _dsa_prompt_kernel,
        grid=(batch, NQT),
        in_specs=[pl.BlockSpec((TQ, D_ATT), lambda b, i: (b * NQT + i, C_Q // D_ATT)),
                  pl.BlockSpec((TQ, N_IDX * IDX_DIM), lambda b, i: (b * NQT + i, C_QI // (N_IDX * IDX_DIM))),
                  pl.BlockSpec((SEQ, N_KV * HEAD_DIM), lambda b, i: (b, C_K // (N_KV * HEAD_DIM))),
                  pl.BlockSpec((SEQ, N_KV * HEAD_DIM), lambda b, i: (b, C_V // (N_KV * HEAD_DIM))),
                  pl.BlockSpec((SEQ, LANES), lambda b, i: (b, C_KW // LANES)),
                  pl.BlockSpec((TQ, LANES), lambda b, i: (b * NQT + i, C_KW // LANES))],
        out_specs=pl.BlockSpec((TQ, D_ATT), lambda b, i: (b * NQT + i, 0)),
        out_shape=jax.ShapeDtypeStruct((t, D_ATT), F32),
        scratch_shapes=[pltpu.VMEM((N_IDX, TQ, IDX_DIM), BF16),
                        pltpu.VMEM((N_IDX, TQ, LANES), F32),
                        pltpu.VMEM((NQT, TQ, TQ), I32),
                        pltpu.VMEM((N_HEADS, TQ, LANES), F32),
                        pltpu.VMEM((N_HEADS, TQ, LANES), F32),
                        pltpu.VMEM((N_HEADS, TQ, HEAD_DIM), F32)],
        compiler_params=_cparams(("arbitrary", "arbitrary")),
        name="dsa_prompt",
    )(zb, zb, zb, zb, zb, z)


N_KPAGES = N_PAGES + 1
LKP = N_KPAGES * PAGE
Q_ROWS = DEC_SEQ * N_HEADS
QI_ROWS = DEC_SEQ * N_IDX


def _dsa_sample_kernel(pt_ref, q_ref, qis_ref, wib_ref, *rest):
    del pt_ref
    ki_refs = rest[:N_KPAGES]
    k_refs = rest[N_KPAGES:2 * N_KPAGES]
    v_refs = rest[2 * N_KPAGES:3 * N_KPAGES]
    o_ref, s_ref, a_ref = rest[3 * N_KPAGES:]

    qis = qis_ref[...]
    wib = wib_ref[...] * W_IDX_SCALE
    s_ref[...] = jnp.full(s_ref.shape, -jnp.inf, F32)
    for j in range(N_KPAGES):
        z = lax.dot_general(qis, ki_refs[j][...].astype(BF16), NT, preferred_element_type=F32)
        rw = jnp.maximum(z, 0.0) * wib
        for t in range(DEC_SEQ):
            s_ref[t:t + 1, j * PAGE:(j + 1) * PAGE] = jnp.sum(rw[t * N_IDX:(t + 1) * N_IDX], axis=0, keepdims=True)

    trow = lax.broadcasted_iota(I32, (SUBLANES, LKP), 0)
    kpos = lax.broadcasted_iota(I32, (SUBLANES, LKP), 1)
    admissible = (kpos <= PAST_LEN + trow) & (trow < DEC_SEQ)
    keys = _sort_key(jnp.where(admissible, s_ref[...], -jnp.inf))

    def bit_body(bi, tu):
        cand_u = tu | jnp.left_shift(jnp.int32(1), 31 - bi)
        cnt = jnp.sum(jnp.where(keys >= (cand_u ^ INT_MIN), 1.0, 0.0), axis=1, keepdims=True)
        return jnp.where(cnt >= TOPK, cand_u, tu)

    tu = lax.fori_loop(0, 32, bit_body, jnp.zeros((SUBLANES, 1), I32))
    thr = jnp.maximum(tu ^ INT_MIN, KEY_OF_NEG_INF + 1)
    sel = jnp.where(keys >= thr, 1.0, 0.0)

    q = q_ref[...]
    arow = lax.broadcasted_iota(I32, (Q_ROWS, PAGE), 0)
    first_group = (arow % N_HEADS) < GQA
    for j in range(N_KPAGES):
        kb = k_refs[j][...].astype(BF16)
        s0 = lax.dot_general(q, kb[:, :HEAD_DIM], NT, preferred_element_type=F32)
        s1 = lax.dot_general(q, kb[:, HEAD_DIM:], NT, preferred_element_type=F32)
        a_ref[:, j * PAGE:(j + 1) * PAGE] = jnp.where(first_group, s0, s1)

    arow = lax.broadcasted_iota(I32, (Q_ROWS, LKP), 0)
    acol = lax.broadcasted_iota(I32, (Q_ROWS, LKP), 1)
    head = arow % N_HEADS
    slope = lax.bitcast_convert_type((126 - head) << 23, F32)
    dist = (PAST_LEN + arow // N_HEADS - acol).astype(F32)
    s = a_ref[...] * ATT_SCALE - slope * dist
    selb = jnp.concatenate([jnp.broadcast_to(sel[t:t + 1], (N_HEADS, LKP)) for t in range(DEC_SEQ)], axis=0)
    s = jnp.where(selb > 0.5, s, NEG)
    m = jnp.max(s, axis=1, keepdims=True)
    p = jnp.exp(s - m)
    l = jnp.sum(p, axis=1, keepdims=True)
    pb = p.astype(BF16)
    acc = jnp.zeros((Q_ROWS, N_KV * HEAD_DIM), F32)
    for j in range(N_KPAGES):
        acc = acc + jnp.dot(pb[:, j * PAGE:(j + 1) * PAGE], v_refs[j][...].astype(BF16), preferred_element_type=F32)
    first_group = (lax.broadcasted_iota(I32, (Q_ROWS, HEAD_DIM), 0) % N_HEADS) < GQA
    o_ref[...] = jnp.where(first_group, acc[:, :HEAD_DIM], acc[:, HEAD_DIM:]) / l


def _dsa_sample(page_table, q_rows, qis_rows, wib_rows, kidx_pool, k_pool, v_pool, ki_new, k_new, v_new):
    nb = page_table.shape[0]

    def pool_spec(width, j):
        return pl.BlockSpec((None, PAGE, width), lambda b, pt: (pt[b, j], 0, 0))

    def new_spec(width):
        return pl.BlockSpec((None, PAGE, width), lambda b, pt: (b, 0, 0))

    in_specs = [pl.BlockSpec((Q_ROWS, HEAD_DIM), lambda b, pt: (b, 0)),
                pl.BlockSpec((QI_ROWS, IDX_DIM), lambda b, pt: (b, 0)),
                pl.BlockSpec((QI_ROWS, LANES), lambda b, pt: (b, 0))]
    args = [q_rows, qis_rows, wib_rows]
    for pool, new, width in ((kidx_pool, ki_new, IDX_DIM), (k_pool, k_new, N_KV * HEAD_DIM),
                             (v_pool, v_new, N_KV * HEAD_DIM)):
        in_specs += [pool_spec(width, j) for j in range(N_PAGES)] + [new_spec(width)]
        args += [pool] * N_PAGES + [new]
    return pl.pallas_call(
        _dsa_sample_kernel,
        grid_spec=pltpu.PrefetchScalarGridSpec(
            num_scalar_prefetch=1,
            grid=(nb,),
            in_specs=in_specs,
            out_specs=pl.BlockSpec((Q_ROWS, HEAD_DIM), lambda b, pt: (b, 0)),
            scratch_shapes=[pltpu.VMEM((SUBLANES, LKP), F32), pltpu.VMEM((Q_ROWS, LKP), F32)]),
        out_shape=jax.ShapeDtypeStruct((nb * Q_ROWS, HEAD_DIM), F32),
        compiler_params=_cparams(("arbitrary",)),
        name="dsa_sample",
    )(page_table, *args)


SEGS = SUBLANES
SEG_LEN = 64
CHUNK = SEGS * SEG_LEN
HALF = N_STATE // 2


def _ssm_params_kernel(lr_ref, li_ref, ldt_ref, bre_ref, bim_ref, cre_ref, cim_ref,
                       wb_ref, wc_ref, a_ref, aseg_ref):
    lr, li = lr_ref[...], li_ref[...]
    dt = jnp.exp(ldt_ref[...])
    mag = jnp.exp(lr * dt)
    ar, ai = mag * jnp.cos(li * dt), mag * jnp.sin(li * dt)
    den = lr * lr + li * li
    cr = (lr * (ar - 1.0) + li * ai) / den
    ci = (lr * ai - li * (ar - 1.0)) / den
    a_ref[0:1, :] = ar
    a_ref[1:2, :] = ai
    pr, pi = ar, ai
    for _ in range(6):
        pr, pi = pr * pr - pi * pi, 2.0 * pr * pi
    aseg_ref[0:1, :] = pr
    aseg_ref[1:2, :] = pi

    brow = lax.broadcasted_iota(I32, (D_SSM, N_STATE), 0) // SSM_GROUP
    bcol = lax.broadcasted_iota(I32, (D_SSM, N_STATE), 1) // SSM_STATE
    same = brow == bcol
    br, bi = bre_ref[...], bim_ref[...]
    wb_ref[:, :N_STATE] = jnp.where(same, cr * br - ci * bi, 0.0).astype(BF16)
    wb_ref[:, N_STATE:] = jnp.where(same, cr * bi + ci * br, 0.0).astype(BF16)

    crow = lax.broadcasted_iota(I32, (N_STATE, D_SSM), 0) // SSM_STATE
    ccol = lax.broadcasted_iota(I32, (N_STATE, D_SSM), 1) // SSM_GROUP
    same = crow == ccol
    wc_ref[:N_STATE, :] = jnp.where(same, cre_ref[...], 0.0).astype(BF16)
    wc_ref[N_STATE:, :] = jnp.where(same, -cim_ref[...], 0.0).astype(BF16)


def _ssm_params(lam_re, lam_im, log_dt, b_re, b_im, c_re, c_im):
    flat = lambda a: a.reshape(1, N_STATE)
    ldt = jnp.repeat(log_dt, SSM_STATE).reshape(1, N_STATE)
    tile_b = lambda b: jnp.tile(b.transpose(0, 2, 1).reshape(D_SSM, SSM_STATE), (1, N_GROUPS))
    tile_c = lambda c: jnp.tile(c.transpose(0, 2, 1).reshape(N_STATE, SSM_GROUP), (1, N_GROUPS))
    return pl.pallas_call(
        _ssm_params_kernel,
        out_shape=[jax.ShapeDtypeStruct((D_SSM, 2 * N_STATE), BF16),
                   jax.ShapeDtypeStruct((2 * N_STATE, D_SSM), BF16),
                   jax.ShapeDtypeStruct((2, N_STATE), F32),
                   jax.ShapeDtypeStruct((2, N_STATE), F32)],
        compiler_params=pltpu.CompilerParams(vmem_limit_bytes=VMEM_LIMIT),
        name="ssm_params",
    )(flat(lam_re), flat(lam_im), ldt, tile_b(b_re), tile_b(b_im), tile_c(c_re), tile_c(c_im))


def _s5_tail(hr_b, hi_b, u, wc_ref, d_ref, wglu_ref):
    y = (jnp.dot(hr_b, wc_ref[:N_STATE, :], preferred_element_type=F32)
         + jnp.dot(hi_b, wc_ref[N_STATE:, :], preferred_element_type=F32)
         + d_ref[...] * u)
    y = jax.nn.gelu(y)
    yg = jnp.dot(y.astype(BF16), wglu_ref[...], preferred_element_type=F32)
    return yg[:, :D_SSM] * _sigmoid(yg[:, D_SSM:])


def _s5_prompt_kernel(u_ref, wb_ref, a_ref, aseg_ref, wc_ref, d_ref, wglu_ref,
                      o_ref, hr_out, hi_out, xr_ref, xi_ref, fin_r, fin_i, cin_r, cin_i, car_r, car_i):
    @pl.when(pl.program_id(1) == 0)
    def _():
        car_r[...] = jnp.zeros(car_r.shape, F32)
        car_i[...] = jnp.zeros(car_i.shape, F32)

    u = u_ref[...]
    ub = u.astype(BF16)
    xr_ref[...] = jnp.dot(ub, wb_ref[:, :N_STATE], preferred_element_type=F32)
    xi_ref[...] = jnp.dot(ub, wb_ref[:, N_STATE:], preferred_element_type=F32)

    for half in range(2):
        ls = slice(half * HALF, (half + 1) * HALF)
        ar = jnp.broadcast_to(a_ref[0:1, ls], (SEGS, HALF))
        ai = jnp.broadcast_to(a_ref[1:2, ls], (SEGS, HALF))

        def local(t, h):
            hr, hi = h
            r0 = pl.multiple_of(t * SEGS, SEGS)
            return (ar * hr - ai * hi + xr_ref[pl.ds(r0, SEGS), ls],
                    ar * hi + ai * hr + xi_ref[pl.ds(r0, SEGS), ls])

        zero = jnp.zeros((SEGS, HALF), F32)
        fr, fi = lax.fori_loop(0, SEG_LEN, local, (zero, zero))
        fin_r[:, ls] = fr
        fin_i[:, ls] = fi

        pr, pi = aseg_ref[0:1, ls], aseg_ref[1:2, ls]
        cr, ci = car_r[:, ls], car_i[:, ls]
        for s in range(SEGS):
            cin_r[s:s + 1, ls] = cr
            cin_i[s:s + 1, ls] = ci
            fr_s, fi_s = fin_r[s:s + 1, ls], fin_i[s:s + 1, ls]
            cr, ci = pr * cr - pi * ci + fr_s, pr * ci + pi * cr + fi_s
        car_r[:, ls] = cr
        car_i[:, ls] = ci

        def full(t, h):
            hr, hi = h
            r0 = pl.multiple_of(t * SEGS, SEGS)
            nr = ar * hr - ai * hi + xr_ref[pl.ds(r0, SEGS), ls]
            ni = ar * hi + ai * hr + xi_ref[pl.ds(r0, SEGS), ls]
            xr_ref[pl.ds(r0, SEGS), ls] = nr
            xi_ref[pl.ds(r0, SEGS), ls] = ni
            return nr, ni

        lax.fori_loop(0, SEG_LEN, full, (cin_r[:, ls], cin_i[:, ls]))

    hr_out[...] = xr_ref[CHUNK - SEGS:, :]
    hi_out[...] = xi_ref[CHUNK - SEGS:, :]
    o_ref[...] = _s5_tail(xr_ref[...].astype(BF16), xi_ref[...].astype(BF16), u, wc_ref, d_ref, wglu_ref)


def _s5_prompt(u_perm, wb, a, aseg, wc, d_flat, wglu_bf16, batch):
    nch = SEQ // CHUNK
    const = lambda shape: pl.BlockSpec(shape, lambda b, c: (0, 0))
    return pl.pallas_call(
        _s5_prompt_kernel,
        grid=(batch, nch),
        in_specs=[pl.BlockSpec((CHUNK, D_SSM), lambda b, c: (b * nch + c, 0)),
                  const((D_SSM, 2 * N_STATE)), const((2, N_STATE)), const((2, N_STATE)),
                  const((2 * N_STATE, D_SSM)), const((1, D_SSM)), const((D_SSM, 2 * D_SSM))],
        out_specs=[pl.BlockSpec((CHUNK, D_SSM), lambda b, c: (b * nch + c, 0)),
                   pl.BlockSpec((None, SEGS, N_STATE), lambda b, c: (b, 0, 0)),
                   pl.BlockSpec((None, SEGS, N_STATE), lambda b, c: (b, 0, 0))],
        out_shape=[jax.ShapeDtypeStruct((batch * SEQ, D_SSM), F32),
                   jax.ShapeDtypeStruct((batch, SEGS, N_STATE), F32),
                   jax.ShapeDtypeStruct((batch, SEGS, N_STATE), F32)],
        scratch_shapes=[pltpu.VMEM((CHUNK, N_STATE), F32), pltpu.VMEM((CHUNK, N_STATE), F32),
                        pltpu.VMEM((SEGS, N_STATE), F32), pltpu.VMEM((SEGS, N_STATE), F32),
                        pltpu.VMEM((SEGS, N_STATE), F32), pltpu.VMEM((SEGS, N_STATE), F32),
                        pltpu.VMEM((1, N_STATE), F32), pltpu.VMEM((1, N_STATE), F32)],
        compiler_params=_cparams(("arbitrary", "arbitrary")),
        name="s5_prompt",
    )(u_perm, wb, a, aseg, wc, d_flat, wglu_bf16)


def _s5_sample_kernel(u_ref, h0r_ref, h0i_ref, wb_ref, a_ref, wc_ref, d_ref, wglu_ref,
                      o_ref, hr_out, hi_out, xr_ref, xi_ref):
    nb = h0r_ref.shape[0]
    u = u_ref[...]
    u_hi = u.astype(BF16)
    u_lo = (u - u_hi.astype(F32)).astype(BF16)
    xr_ref[...] = (jnp.dot(u_hi, wb_ref[:, :N_STATE], preferred_element_type=F32)
                   + jnp.dot(u_lo, wb_ref[:, :N_STATE], preferred_element_type=F32))
    xi_ref[...] = (jnp.dot(u_hi, wb_ref[:, N_STATE:], preferred_element_type=F32)
                   + jnp.dot(u_lo, wb_ref[:, N_STATE:], preferred_element_type=F32))
    ar, ai = a_ref[0:1, :], a_ref[1:2, :]
    hr, hi = h0r_ref[...], h0i_ref[...]
    for t in range(DEC_SEQ):
        rows = slice(t * nb, (t + 1) * nb)
        hr, hi = ar * hr - ai * hi + xr_ref[rows, :], ar * hi + ai * hr + xi_ref[rows, :]
        xr_ref[rows, :] = hr
        xi_ref[rows, :] = hi
    hr_out[...] = hr
    hi_out[...] = hi
    o_ref[...] = _s5_tail(xr_ref[...].astype(BF16), xi_ref[...].astype(BF16), u, wc_ref, d_ref, wglu_ref)


def _s5_sample(u_tb, h0r, h0i, wb, a, wc, d_flat, wglu_bf16):
    rows = u_tb.shape[0]
    nb = h0r.shape[0]
    return pl.pallas_call(
        _s5_sample_kernel,
        out_shape=[jax.ShapeDtypeStruct((rows, D_SSM), F32),
                   jax.ShapeDtypeStruct((nb, N_STATE), F32),
                   jax.ShapeDtypeStruct((nb, N_STATE), F32)],
        scratch_shapes=[pltpu.VMEM((rows, N_STATE), F32), pltpu.VMEM((rows, N_STATE), F32)],
        compiler_params=pltpu.CompilerParams(vmem_limit_bytes=VMEM_LIMIT),
        name="s5_sample",
    )(u_tb, h0r, h0i, wb, a, wc, d_flat, wglu_bf16)


TQM = 512


def _mem_prompt_kernel(q_ref, k_ref, v_ref, o_ref):
    for h in range(N_MEM_HEADS):
        cs = slice(h * HEAD_DIM, (h + 1) * HEAD_DIM)
        s = lax.dot_general(q_ref[:, cs], k_ref[:, cs], NT, preferred_element_type=F32) * ATT_SCALE
        p = jnp.exp(s - jnp.max(s, axis=1, keepdims=True))
        l = jnp.sum(p, axis=1, keepdims=True)
        o_ref[:, cs] = jnp.dot(p.astype(BF16), v_ref[:, cs], preferred_element_type=F32) / l


def _mem_prompt(zb, kvb, batch):
    nqt = SEQ // TQM
    return pl.pallas_call(
        _mem_prompt_kernel,
        grid=(batch, nqt),
        in_specs=[pl.BlockSpec((TQM, D_MEM), lambda b, i: (b * nqt + i, C_QM // D_MEM)),
                  pl.BlockSpec((N_MEM, D_MEM), lambda b, i: (b, 0)),
                  pl.BlockSpec((N_MEM, D_MEM), lambda b, i: (b, 1))],
        out_specs=pl.BlockSpec((TQM, D_MEM), lambda b, i: (b * nqt + i, 0)),
        out_shape=jax.ShapeDtypeStruct((batch * SEQ, D_MEM), F32),
        compiler_params=_cparams(("arbitrary", "arbitrary")),
        name="mem_prompt",
    )(zb, kvb, kvb)


MEM_G = 8
MEM_ROWS = N_MEM_HEADS * DEC_SEQ


def _mem_sample_kernel(q_ref, k_ref, v_ref, o_ref):
    row = lax.broadcasted_iota(I32, (MEM_ROWS, D_MEM), 0)
    col = lax.broadcasted_iota(I32, (MEM_ROWS, D_MEM), 1)
    own = (row // DEC_SEQ) == (col // HEAD_DIM)
    for bb in range(MEM_G):
        rs = slice(bb * MEM_ROWS, (bb + 1) * MEM_ROWS)
        qblk = jnp.where(own, q_ref[rs, :], 0.0).astype(BF16)
        s = lax.dot_general(qblk, k_ref[bb].astype(BF16), NT, preferred_element_type=F32) * ATT_SCALE
        p = jnp.exp(s - jnp.max(s, axis=1, keepdims=True))
        l = jnp.sum(p, axis=1, keepdims=True)
        o = jnp.dot(p.astype(BF16), v_ref[bb].astype(BF16), preferred_element_type=F32) / l
        o_ref[rs, :] = jnp.where(own, o, 0.0)


def _mem_sample(q_rep, mem_k, mem_v):
    nb = mem_k.shape[0]
    return pl.pallas_call(
        _mem_sample_kernel,
        grid=(nb // MEM_G,),
        in_specs=[pl.BlockSpec((MEM_G * MEM_ROWS, D_MEM), lambda i: (i, 0)),
                  pl.BlockSpec((MEM_G, N_MEM, D_MEM), lambda i: (i, 0, 0)),
                  pl.BlockSpec((MEM_G, N_MEM, D_MEM), lambda i: (i, 0, 0))],
        out_specs=pl.BlockSpec((MEM_G * MEM_ROWS, D_MEM), lambda i: (i, 0)),
        out_shape=jax.ShapeDtypeStruct((nb * MEM_ROWS, D_MEM), F32),
        compiler_params=_cparams(("arbitrary",)),
        name="mem_sample",
    )(q_rep, mem_k, mem_v)


def _out_kernel(x_ref, oa_ref, os_ref, om_ref, ga_ref, gs_ref, gm_ref, w_ref, g_ref, y_ref):
    def gated(o_ref, gate_ref):
        gate = gate_ref[...]
        return (o_ref[...] * (gate * _sigmoid(gate))).astype(BF16)

    y = (x_ref[...]
         + jnp.dot(gated(oa_ref, ga_ref), w_ref[:D_ATT, :], preferred_element_type=F32)
         + jnp.dot(gated(os_ref, gs_ref), w_ref[D_ATT:D_ATT + D_SSM, :], preferred_element_type=F32)
         + jnp.dot(gated(om_ref, gm_ref), w_ref[D_ATT + D_SSM:, :], preferred_element_type=F32))
    ms = jnp.mean(y * y, axis=-1, keepdims=True)
    y_ref[...] = y * lax.rsqrt(ms + EPS) * g_ref[...]


def _out_proj(x, o_a, o_s, o_m, z, w_out_bf16, g_final, tm):
    m = x.shape[0]
    rows = lambda w, c: pl.BlockSpec((tm, w), lambda i: (i, c))
    return pl.pallas_call(
        _out_kernel,
        grid=(m // tm,),
        in_specs=[rows(D_MODEL, 0), rows(D_ATT, 0), rows(D_SSM, 0), rows(D_MEM, 0),
                  rows(D_ATT, C_GA // D_ATT), rows(D_SSM, C_GS // D_SSM), rows(D_MEM, C_GM // D_MEM),
                  pl.BlockSpec((D_MODEL, D_MODEL), lambda i: (0, 0)),
                  pl.BlockSpec((1, D_MODEL), lambda i: (0, 0))],
        out_specs=rows(D_MODEL, 0),
        out_shape=jax.ShapeDtypeStruct((m, D_MODEL), F32),
        compiler_params=_cparams(("arbitrary",)),
        name="out_proj",
    )(x, o_a, o_s, o_m, z, z, z, w_out_bf16, g_final.reshape(1, D_MODEL))


def _reorder_w_in(w):
    q, k, v, ga, qi, ki, wi, u, gs, qm, gm = jnp.split(
        w, [1024, 1280, 1536, 2560, 3584, 3648, 3664, 4176, 4688, 5200], axis=1)
    pad = jnp.zeros((w.shape[0], D_INP - w.shape[1]), w.dtype)
    return jnp.concatenate([q, qi, ga, k, v, u, gs, qm, gm, ki, wi, pad], axis=1).astype(BF16)


def kernel(x_prompt, x_sample, mem_prompt, cache_k, cache_v, cache_kidx, cache_mem_k, cache_mem_v,
           state_ssm_re, state_ssm_im, page_table, g_norm, w_in, g_mem, w_mem_kv, lam_re, lam_im,
           log_dt, b_re, b_im, c_re, c_im, d_skip, w_glu, w_out, g_final):
    depth = w_in.shape[0]
    assert depth == 1
    l = 0
    bp, db = x_prompt.shape[0], x_sample.shape[0]
    tp, ts = bp * SEQ, db * DEC_SEQ

    w_in_b = _reorder_w_in(w_in[l])
    wb, wc, a, aseg = _ssm_params(lam_re[l], lam_im[l], log_dt[l], b_re[l], b_im[l], c_re[l], c_im[l])
    d_flat = d_skip[l].reshape(1, D_SSM)
    wglu_b = w_glu[l].astype(BF16)
    w_out_b = w_out[l].astype(BF16)

    xp = x_prompt.reshape(tp, D_MODEL)
    xs = x_sample.reshape(ts, D_MODEL)
    zp, zpb = _proj(xp, g_norm[l], w_in_b, 1024, 640)
    zs, zsb = _proj(xs, g_norm[l], w_in_b, 512, 640)
    kv, kvb = _proj(mem_prompt.reshape(bp * N_MEM, D_MODEL), g_mem[l], w_mem_kv[l].astype(BF16), 512, 512)

    oa_p = _dsa_prompt(zp, zpb, bp)
    nch = SEQ // CHUNK
    u_p = zp[:, C_U:C_U + D_SSM].reshape(bp, nch, SEGS, SEG_LEN, D_SSM).transpose(0, 1, 3, 2, 4).reshape(tp, D_SSM)
    os_perm, hr_p, hi_p = _s5_prompt(u_p, wb, a, aseg, wc, d_flat, wglu_b, bp)
    os_p = os_perm.reshape(bp, nch, SEG_LEN, SEGS, D_SSM).transpose(0, 1, 3, 2, 4).reshape(tp, D_SSM)
    om_p = _mem_prompt(zpb, kvb, bp)
    y_p = _out_proj(xp, oa_p, os_p, om_p, zp, w_out_b, g_final, 256)

    pool = cache_k.shape[1]
    q_rows = zsb[:, C_Q:C_Q + D_ATT].reshape(ts * N_HEADS, HEAD_DIM)
    qis_rows = zsb[:, C_QI:C_QI + N_IDX * IDX_DIM].reshape(ts * N_IDX, IDX_DIM)
    wib_rows = jnp.broadcast_to(zs[:, C_KW + IDX_DIM:C_KW + IDX_DIM + N_IDX].reshape(ts * N_IDX, 1),
                                (ts * N_IDX, LANES))

    def new_page(cols, width):
        rows = zs[:, cols:cols + width].reshape(db, DEC_SEQ, width)
        return jnp.pad(rows, ((0, 0), (0, PAGE - DEC_SEQ), (0, 0)))

    oa_rows = _dsa_sample(page_table, q_rows, qis_rows, wib_rows,
                          cache_kidx[l].reshape(pool, PAGE, IDX_DIM),
                          cache_k[l].reshape(pool, PAGE, N_KV * HEAD_DIM),
                          cache_v[l].reshape(pool, PAGE, N_KV * HEAD_DIM),
                          new_page(C_KW, IDX_DIM), new_page(C_K, N_KV * HEAD_DIM), new_page(C_V, N_KV * HEAD_DIM))
    oa_s = oa_rows.reshape(ts, D_ATT)

    u_tb = zs[:, C_U:C_U + D_SSM].reshape(db, DEC_SEQ, D_SSM).transpose(1, 0, 2).reshape(ts, D_SSM)
    os_tb, hr_s, hi_s = _s5_sample(u_tb, state_ssm_re[l].reshape(db, N_STATE), state_ssm_im[l].reshape(db, N_STATE),
                                   wb, a, wc, d_flat, wglu_b)
    os_s = os_tb.reshape(DEC_SEQ, db, D_SSM).transpose(1, 0, 2).reshape(ts, D_SSM)

    qm = zs[:, C_QM:C_QM + D_MEM].reshape(db, 1, DEC_SEQ, D_MEM)
    q_rep = jnp.broadcast_to(qm, (db, N_MEM_HEADS, DEC_SEQ, D_MEM)).reshape(db * MEM_ROWS, D_MEM)
    om_blk = _mem_sample(q_rep, cache_mem_k[l].reshape(db, N_MEM, D_MEM), cache_mem_v[l].reshape(db, N_MEM, D_MEM))
    om_s = om_blk.reshape(db, N_MEM_HEADS, DEC_SEQ, D_MEM).sum(axis=1).reshape(ts, D_MEM)
    y_s = _out_proj(xs, oa_s, os_s, om_s, zs, w_out_b, g_final, 256)

    kvshape = (depth, bp, SEQ, N_KV, HEAD_DIM)
    skv = (depth, db, DEC_SEQ, N_KV, HEAD_DIM)
    width = N_KV * HEAD_DIM
    return (y_p.reshape(bp, SEQ, D_MODEL), y_s.reshape(db, DEC_SEQ, D_MODEL),
            zp[:, C_K:C_K + width].reshape(kvshape), zp[:, C_V:C_V + width].reshape(kvshape),
            zp[:, C_KW:C_KW + IDX_DIM].reshape(depth, bp, SEQ, IDX_DIM),
            kv[:, :D_MEM].reshape(depth, bp, N_MEM, N_MEM_HEADS, HEAD_DIM),
            kv[:, D_MEM:].reshape(depth, bp, N_MEM, N_MEM_HEADS, HEAD_DIM),
            hr_p[:, SEGS - 1].reshape(depth, bp, N_GROUPS, SSM_STATE),
            hi_p[:, SEGS - 1].reshape(depth, bp, N_GROUPS, SSM_STATE),
            zs[:, C_K:C_K + width].reshape(skv), zs[:, C_V:C_V + width].reshape(skv),
            zs[:, C_KW:C_KW + IDX_DIM].reshape(depth, db, DEC_SEQ, IDX_DIM),
            hr_s.reshape(depth, db, N_GROUPS, SSM_STATE), hi_s.reshape(depth, db, N_GROUPS, SSM_STATE))
```

```python
import functools

import jax
import jax.numpy as jnp
from jax import lax
from jax.experimental import pallas as pl
from jax.experimental.pallas import tpu as pltpu

F32 = jnp.float32
BF16 = jnp.bfloat16
I32 = jnp.int32

D_MODEL = 2048
SEQ = 4096
DEC_SEQ = 4
PAST_LEN = 2048
PAGE = 128
N_PAGES = PAST_LEN // PAGE
D_ATT = 1024
D_SSM = 512
D_MEM = 512
HEAD_DIM = 128
N_HEADS = 8
N_KV = 2
GQA = N_HEADS // N_KV
N_IDX = 16
IDX_DIM = 64
TOPK = 256
SSM_GROUP = 16
N_GROUPS = 32
SSM_STATE = 64
N_STATE = N_GROUPS * SSM_STATE
N_MEM = 256
N_MEM_HEADS = 4
EPS = 1e-6

LANES = 128
SUBLANES = 8
VMEM_LIMIT = 56 * 1024 * 1024

C_Q, C_QI, C_GA, C_K, C_V, C_U, C_GS, C_QM, C_GM, C_KW = 0, 1024, 2048, 3072, 3328, 3584, 4096, 4608, 5120, 5632
D_INP = 5760
W_IDX_SCALE = (N_IDX ** -0.5) * (IDX_DIM ** -0.5)
ATT_SCALE = HEAD_DIM ** -0.5
NEG = -1e30
INT_MIN = -2 ** 31
KEY_OF_NEG_INF = -2139095041
NT = (((1,), (1,)), ((), ()))


def _cparams(sem):
    return pltpu.CompilerParams(dimension_semantics=sem, vmem_limit_bytes=VMEM_LIMIT)


def _sort_key(s):
    b = lax.bitcast_convert_type(s, I32)
    return b ^ ((b >> 31) & 0x7FFFFFFF)


def _sigmoid(x):
    return 1.0 / (1.0 + jnp.exp(-x))


def _proj_kernel(x_ref, g_ref, w_ref, o_ref, ob_ref, h_ref):
    @pl.when(pl.program_id(1) == 0)
    def _():
        x = x_ref[...]
        ms = jnp.mean(x * x, axis=-1, keepdims=True)
        h_ref[...] = (x * lax.rsqrt(ms + EPS) * g_ref[...]).astype(BF16)

    z = jnp.dot(h_ref[...], w_ref[...], preferred_element_type=F32)
    o_ref[...] = z
    ob_ref[...] = z.astype(BF16)


def _proj(x, g, w_bf16, tm, tn):
    m, d = x.shape
    n = w_bf16.shape[1]
    return pl.pallas_call(
        _proj_kernel,
        grid=(m // tm, n // tn),
        in_specs=[pl.BlockSpec((tm, d), lambda i, j: (i, 0)),
                  pl.BlockSpec((1, d), lambda i, j: (0, 0)),
                  pl.BlockSpec((d, tn), lambda i, j: (0, j))],
        out_specs=[pl.BlockSpec((tm, tn), lambda i, j: (i, j)),
                   pl.BlockSpec((tm, tn), lambda i, j: (i, j))],
        out_shape=[jax.ShapeDtypeStruct((m, n), F32), jax.ShapeDtypeStruct((m, n), BF16)],
        scratch_shapes=[pltpu.VMEM((tm, d), BF16)],
        compiler_params=_cparams(("arbitrary", "arbitrary")),
        name="proj",
    )(x, g.reshape(1, d), w_bf16)


TQ = 256
NQT = SEQ // TQ


def _dsa_prompt_kernel(q_ref, qi_ref, k_ref, v_ref, kwk_ref, kwq_ref, o_ref,
                       qis_ref, wib_ref, sc_ref, m_ref, l_ref, acc_ref):
    i = pl.program_id(1)
    row = lax.broadcasted_iota(I32, (TQ, TQ), 0)
    col = lax.broadcasted_iota(I32, (TQ, TQ), 1)

    for h in range(N_IDX):
        qis_ref[h] = qi_ref[:, h * IDX_DIM:(h + 1) * IDX_DIM]
        w = kwq_ref[:, IDX_DIM + h:IDX_DIM + h + 1] * W_IDX_SCALE
        wib_ref[h] = jnp.broadcast_to(w, (TQ, LANES))

    def score_tile(kt):
        kk = kwk_ref[pl.ds(pl.multiple_of(kt * TQ, TQ), TQ), :][:, :IDX_DIM]
        acc = jnp.zeros((TQ, TQ), F32)
        for h in range(N_IDX):
            z = lax.dot_general(qis_ref[h], kk, NT, preferred_element_type=F32)
            acc = acc + jnp.tile(wib_ref[h], (1, TQ // LANES)) * jnp.maximum(z, 0.0)
        return acc

    def fill(kt, carry):
        sc_ref[kt] = _sort_key(score_tile(kt))
        return carry

    lax.fori_loop(0, i, fill, 0)
    sc_ref[i] = _sort_key(jnp.where(col <= row, score_tile(i), -jnp.inf))

    def bit_body(bi, tu):
        cand_u = tu | jnp.left_shift(jnp.int32(1), 31 - bi)
        cand_s = jnp.tile(cand_u ^ INT_MIN, (1, TQ // LANES))

        def count(kt, acc):
            hit = jnp.where(sc_ref[kt] >= cand_s, 1.0, 0.0)
            return acc + hit[:, :LANES] + hit[:, LANES:]

        acc = lax.fori_loop(0, i + 1, count, jnp.zeros((TQ, LANES), F32))
        cnt = jnp.sum(acc, axis=1, keepdims=True)
        return jnp.where(cnt >= TOPK, cand_u, tu)

    tu = lax.fori_loop(0, 32, bit_body, jnp.zeros((TQ, LANES), I32))
    thr = jnp.tile(jnp.maximum(tu ^ INT_MIN, KEY_OF_NEG_INF + 1), (1, TQ // LANES))

    m_ref[...] = jnp.full(m_ref.shape, NEG, F32)
    l_ref[...] = jnp.zeros(l_ref.shape, F32)
    acc_ref[...] = jnp.zeros(acc_ref.shape, F32)

    def attend(kt, carry):
        mask = sc_ref[kt] >= thr
        dist = ((i - kt) * TQ + row - col).astype(F32)
        k0 = pl.multiple_of(kt * TQ, TQ)
        kb = k_ref[pl.ds(k0, TQ), :]
        vb = v_ref[pl.ds(k0, TQ), :]
        for h in range(N_HEADS):
            g = h // GQA
            s = lax.dot_general(q_ref[:, h * HEAD_DIM:(h + 1) * HEAD_DIM], kb[:, g * HEAD_DIM:(g + 1) * HEAD_DIM],
                                NT, preferred_element_type=F32)
            s = s * ATT_SCALE - (2.0 ** -(h + 1)) * dist
            s = jnp.where(mask, s, NEG)
            m_old = m_ref[h]
            m_new = jnp.maximum(m_old, jnp.max(s, axis=1, keepdims=True))
            alpha = jnp.exp(m_old - m_new)
            p = jnp.exp(s - jnp.tile(m_new, (1, TQ // LANES)))
            l_ref[h] = alpha * l_ref[h] + jnp.sum(p, axis=1, keepdims=True)
            acc_ref[h] = alpha * acc_ref[h] + jnp.dot(p.astype(BF16), vb[:, g * HEAD_DIM:(g + 1) * HEAD_DIM],
                                                      preferred_element_type=F32)
            m_ref[h] = m_new
        return carry

    lax.fori_loop(0, i + 1, attend, 0)
    for h in range(N_HEADS):
        o_ref[:, h * HEAD_DIM:(h + 1) * HEAD_DIM] = acc_ref[h] / l_ref[h]


def _dsa_prompt(z, zb, batch):
    t = batch * SEQ
    return pl.pallas_call(
        _dsa_prompt_kernel,
        grid=(batch, NQT),
        in_specs=[pl.BlockSpec((TQ, D_ATT), lambda b, i: (b * NQT + i, C_Q // D_ATT)),
                  pl.BlockSpec((TQ, N_IDX * IDX_DIM), lambda b, i: (b * NQT + i, C_QI // (N_IDX * IDX_DIM))),
                  pl.BlockSpec((SEQ, N_KV * HEAD_DIM), lambda b, i: (b, C_K // (N_KV * HEAD_DIM))),
                  pl.BlockSpec((SEQ, N_KV * HEAD_DIM), lambda b, i: (b, C_V // (N_KV * HEAD_DIM))),
                  pl.BlockSpec((SEQ, LANES), lambda b, i: (b, C_KW // LANES)),
                  pl.BlockSpec((TQ, LANES), lambda b, i: (b * NQT + i, C_KW // LANES))],
        out_specs=pl.BlockSpec((TQ, D_ATT), lambda b, i: (b * NQT + i, 0)),
        out_shape=jax.ShapeDtypeStruct((t, D_ATT), F32),
        scratch_shapes=[pltpu.VMEM((N_IDX, TQ, IDX_DIM), BF16),
                        pltpu.VMEM((N_IDX, TQ, LANES), F32),
                        pltpu.VMEM((NQT, TQ, TQ), I32),
                        pltpu.VMEM((N_HEADS, TQ, LANES), F32),
                        pltpu.VMEM((N_HEADS, TQ, LANES), F32),
                        pltpu.VMEM((N_HEADS, TQ, HEAD_DIM), F32)],
        compiler_params=_cparams(("arbitrary", "arbitrary")),
        name="dsa_prompt",
    )(zb, zb, zb, zb, zb, z)


N_KPAGES = N_PAGES + 1
LKP = N_KPAGES * PAGE
Q_ROWS = DEC_SEQ * N_HEADS
QI_ROWS = DEC_SEQ * N_IDX
SMP_G = 4
S_ROWS = SMP_G * DEC_SEQ


def _dsa_sample_kernel(pt_ref, q_ref, qis_ref, wib_ref, *rest):
    del pt_ref
    n = SMP_G * N_KPAGES
    ki_refs, k_refs, v_refs = rest[:n], rest[n:2 * n], rest[2 * n:3 * n]
    o_ref, s_ref, a_ref = rest[3 * n:]

    for g in range(SMP_G):
        qis = qis_ref[g * QI_ROWS:(g + 1) * QI_ROWS, :]
        wib = wib_ref[g * QI_ROWS:(g + 1) * QI_ROWS, :] * W_IDX_SCALE
        for j in range(N_KPAGES):
            z = lax.dot_general(qis, ki_refs[g * N_KPAGES + j][...].astype(BF16), NT, preferred_element_type=F32)
            rw = jnp.maximum(z, 0.0) * wib
            for t in range(DEC_SEQ):
                r = g * DEC_SEQ + t
                s_ref[r:r + 1, j * PAGE:(j + 1) * PAGE] = jnp.sum(rw[t * N_IDX:(t + 1) * N_IDX], axis=0, keepdims=True)

    trow = lax.broadcasted_iota(I32, (S_ROWS, LKP), 0) % DEC_SEQ
    kpos = lax.broadcasted_iota(I32, (S_ROWS, LKP), 1)
    keys = _sort_key(jnp.where(kpos <= PAST_LEN + trow, s_ref[...], -jnp.inf))

    def bit_body(bi, tu):
        cand_u = tu | jnp.left_shift(jnp.int32(1), 31 - bi)
        cnt = jnp.sum(jnp.where(keys >= (cand_u ^ INT_MIN), 1.0, 0.0), axis=1, keepdims=True)
        return jnp.where(cnt >= TOPK, cand_u, tu)

    tu = lax.fori_loop(0, 32, bit_body, jnp.zeros((S_ROWS, 1), I32))
    thr = jnp.maximum(tu ^ INT_MIN, KEY_OF_NEG_INF + 1)
    sel = jnp.where(keys >= thr, 1.0, 0.0)

    first_group = (lax.broadcasted_iota(I32, (Q_ROWS, PAGE), 0) % N_HEADS) < GQA
    for g in range(SMP_G):
        q = q_ref[g * Q_ROWS:(g + 1) * Q_ROWS, :]
        for j in range(N_KPAGES):
            kref = k_refs[g * N_KPAGES + j]
            s0 = lax.dot_general(q, kref[pl.ds(0, PAGE, stride=N_KV), :].astype(BF16), NT, preferred_element_type=F32)
            s1 = lax.dot_general(q, kref[pl.ds(1, PAGE, stride=N_KV), :].astype(BF16), NT, preferred_element_type=F32)
            a_ref[g * Q_ROWS:(g + 1) * Q_ROWS, j * PAGE:(j + 1) * PAGE] = jnp.where(first_group, s0, s1)

    rows = SMP_G * Q_ROWS
    arow = lax.broadcasted_iota(I32, (rows, LKP), 0)
    acol = lax.broadcasted_iota(I32, (rows, LKP), 1)
    head = arow % N_HEADS
    slope = lax.bitcast_convert_type((126 - head) << 23, F32)
    dist = (PAST_LEN + (arow // N_HEADS) % DEC_SEQ - acol).astype(F32)
    s = a_ref[...] * ATT_SCALE - slope * dist
    selb = jnp.concatenate([jnp.broadcast_to(sel[r:r + 1], (N_HEADS, LKP)) for r in range(S_ROWS)], axis=0)
    s = jnp.where(selb > 0.5, s, NEG)
    m = jnp.max(s, axis=1, keepdims=True)
    p = jnp.exp(s - m)
    l = jnp.sum(p, axis=1, keepdims=True)
    pb = p.astype(BF16)
    first_group = (lax.broadcasted_iota(I32, (Q_ROWS, HEAD_DIM), 0) % N_HEADS) < GQA
    for g in range(SMP_G):
        rs = slice(g * Q_ROWS, (g + 1) * Q_ROWS)
        acc0 = jnp.zeros((Q_ROWS, HEAD_DIM), F32)
        acc1 = jnp.zeros((Q_ROWS, HEAD_DIM), F32)
        for j in range(N_KPAGES):
            vref = v_refs[g * N_KPAGES + j]
            pj = pb[rs, j * PAGE:(j + 1) * PAGE]
            acc0 = acc0 + jnp.dot(pj, vref[pl.ds(0, PAGE, stride=N_KV), :].astype(BF16), preferred_element_type=F32)
            acc1 = acc1 + jnp.dot(pj, vref[pl.ds(1, PAGE, stride=N_KV), :].astype(BF16), preferred_element_type=F32)
        o_ref[rs, :] = jnp.where(first_group, acc0, acc1) / l[rs]


def _dsa_sample(page_table, q_rows, qis_rows, wib_rows, kidx_pool, k_pool, v_pool, ki_new, k_new, v_new):
    nb = page_table.shape[0]

    def pool_spec(rows, width, g, j):
        return pl.BlockSpec((None, rows, width), lambda i, pt: (pt[i * SMP_G + g, j], 0, 0))

    def new_spec(rows, width, g):
        return pl.BlockSpec((None, rows, width), lambda i, pt: (i * SMP_G + g, 0, 0))

    in_specs = [pl.BlockSpec((SMP_G * Q_ROWS, HEAD_DIM), lambda i, pt: (i, 0)),
                pl.BlockSpec((SMP_G * QI_ROWS, IDX_DIM), lambda i, pt: (i, 0)),
                pl.BlockSpec((SMP_G * QI_ROWS, LANES), lambda i, pt: (i, 0))]
    args = [q_rows, qis_rows, wib_rows]
    for pool, new, rows, width in ((kidx_pool, ki_new, PAGE, IDX_DIM), (k_pool, k_new, N_KV * PAGE, HEAD_DIM),
                                   (v_pool, v_new, N_KV * PAGE, HEAD_DIM)):
        for g in range(SMP_G):
            in_specs += [pool_spec(rows, width, g, j) for j in range(N_PAGES)] + [new_spec(rows, width, g)]
            args += [pool] * N_PAGES + [new]
    return pl.pallas_call(
        _dsa_sample_kernel,
        grid_spec=pltpu.PrefetchScalarGridSpec(
            num_scalar_prefetch=1,
            grid=(nb // SMP_G,),
            in_specs=in_specs,
            out_specs=pl.BlockSpec((SMP_G * Q_ROWS, HEAD_DIM), lambda i, pt: (i, 0)),
            scratch_shapes=[pltpu.VMEM((S_ROWS, LKP), F32), pltpu.VMEM((SMP_G * Q_ROWS, LKP), F32)]),
        out_shape=jax.ShapeDtypeStruct((nb * Q_ROWS, HEAD_DIM), F32),
        compiler_params=_cparams(("arbitrary",)),
        name="dsa_sample",
    )(page_table, *args)


SEGS = SUBLANES
SEG_LEN = 64
CHUNK = SEGS * SEG_LEN
HALF = N_STATE // 2


def _ssm_params_kernel(lr_ref, li_ref, ldt_ref, bre_ref, bim_ref, cre_ref, cim_ref,
                       wb_ref, wc_ref, a_ref, aseg_ref):
    lr, li = lr_ref[...], li_ref[...]
    dt = jnp.exp(ldt_ref[...])
    mag = jnp.exp(lr * dt)
    ar, ai = mag * jnp.cos(li * dt), mag * jnp.sin(li * dt)
    den = lr * lr + li * li
    cr = (lr * (ar - 1.0) + li * ai) / den
    ci = (lr * ai - li * (ar - 1.0)) / den
    a_ref[0:1, :] = ar
    a_ref[1:2, :] = ai
    pr, pi = ar, ai
    for _ in range(6):
        pr, pi = pr * pr - pi * pi, 2.0 * pr * pi
    aseg_ref[0:1, :] = pr
    aseg_ref[1:2, :] = pi

    brow = lax.broadcasted_iota(I32, (D_SSM, N_STATE), 0) // SSM_GROUP
    bcol = lax.broadcasted_iota(I32, (D_SSM, N_STATE), 1) // SSM_STATE
    same = brow == bcol
    br, bi = bre_ref[...], bim_ref[...]
    wb_ref[:, :N_STATE] = jnp.where(same, cr * br - ci * bi, 0.0).astype(BF16)
    wb_ref[:, N_STATE:] = jnp.where(same, cr * bi + ci * br, 0.0).astype(BF16)

    crow = lax.broadcasted_iota(I32, (N_STATE, D_SSM), 0) // SSM_STATE
    ccol = lax.broadcasted_iota(I32, (N_STATE, D_SSM), 1) // SSM_GROUP
    same = crow == ccol
    wc_ref[:N_STATE, :] = jnp.where(same, cre_ref[...], 0.0).astype(BF16)
    wc_ref[N_STATE:, :] = jnp.where(same, -cim_ref[...], 0.0).astype(BF16)


def _ssm_params(lam_re, lam_im, log_dt, b_re, b_im, c_re, c_im):
    flat = lambda a: a.reshape(1, N_STATE)
    ldt = jnp.repeat(log_dt, SSM_STATE).reshape(1, N_STATE)
    tile_b = lambda b: jnp.tile(b.transpose(0, 2, 1).reshape(D_SSM, SSM_STATE), (1, N_GROUPS))
    tile_c = lambda c: jnp.tile(c.transpose(0, 2, 1).reshape(N_STATE, SSM_GROUP), (1, N_GROUPS))
    return pl.pallas_call(
        _ssm_params_kernel,
        out_shape=[jax.ShapeDtypeStruct((D_SSM, 2 * N_STATE), BF16),
                   jax.ShapeDtypeStruct((2 * N_STATE, D_SSM), BF16),
                   jax.ShapeDtypeStruct((2, N_STATE), F32),
                   jax.ShapeDtypeStruct((2, N_STATE), F32)],
        compiler_params=pltpu.CompilerParams(vmem_limit_bytes=VMEM_LIMIT),
        name="ssm_params",
    )(flat(lam_re), flat(lam_im), ldt, tile_b(b_re), tile_b(b_im), tile_c(c_re), tile_c(c_im))


def _s5_tail(hr_b, hi_b, u, wc_ref, d_ref, wglu_ref):
    y = (jnp.dot(hr_b, wc_ref[:N_STATE, :], preferred_element_type=F32)
         + jnp.dot(hi_b, wc_ref[N_STATE:, :], preferred_element_type=F32)
         + d_ref[...] * u)
    y = jax.nn.gelu(y)
    yg = jnp.dot(y.astype(BF16), wglu_ref[...], preferred_element_type=F32)
    return yg[:, :D_SSM] * _sigmoid(yg[:, D_SSM:])


def _s5_prompt_kernel(u_ref, wb_ref, a_ref, aseg_ref, wc_ref, d_ref, wglu_ref,
                      o_ref, hr_out, hi_out, xr_ref, xi_ref, fin_r, fin_i, cin_r, cin_i, car_r, car_i):
    @pl.when(pl.program_id(1) == 0)
    def _():
        car_r[...] = jnp.zeros(car_r.shape, F32)
        car_i[...] = jnp.zeros(car_i.shape, F32)

    u = u_ref[...]
    ub = u.astype(BF16)
    xr_ref[...] = jnp.dot(ub, wb_ref[:, :N_STATE], preferred_element_type=F32)
    xi_ref[...] = jnp.dot(ub, wb_ref[:, N_STATE:], preferred_element_type=F32)

    for half in range(2):
        ls = slice(half * HALF, (half + 1) * HALF)
        ar = jnp.broadcast_to(a_ref[0:1, ls], (SEGS, HALF))
        ai = jnp.broadcast_to(a_ref[1:2, ls], (SEGS, HALF))

        def local(t, h):
            hr, hi = h
            r0 = pl.multiple_of(t * SEGS, SEGS)
            return (ar * hr - ai * hi + xr_ref[pl.ds(r0, SEGS), ls],
                    ar * hi + ai * hr + xi_ref[pl.ds(r0, SEGS), ls])

        zero = jnp.zeros((SEGS, HALF), F32)
        fr, fi = lax.fori_loop(0, SEG_LEN, local, (zero, zero))
        fin_r[:, ls] = fr
        fin_i[:, ls] = fi

        pr, pi = aseg_ref[0:1, ls], aseg_ref[1:2, ls]
        cr, ci = car_r[:, ls], car_i[:, ls]
        for s in range(SEGS):
            cin_r[s:s + 1, ls] = cr
            cin_i[s:s + 1, ls] = ci
            fr_s, fi_s = fin_r[s:s + 1, ls], fin_i[s:s + 1, ls]
            cr, ci = pr * cr - pi * ci + fr_s, pr * ci + pi * cr + fi_s
        car_r[:, ls] = cr
        car_i[:, ls] = ci

        def full(t, h):
            hr, hi = h
            r0 = pl.multiple_of(t * SEGS, SEGS)
            nr = ar * hr - ai * hi + xr_ref[pl.ds(r0, SEGS), ls]
            ni = ar * hi + ai * hr + xi_ref[pl.ds(r0, SEGS), ls]
            xr_ref[pl.ds(r0, SEGS), ls] = nr
            xi_ref[pl.ds(r0, SEGS), ls] = ni
            return nr, ni

        lax.fori_loop(0, SEG_LEN, full, (cin_r[:, ls], cin_i[:, ls]))

    hr_out[...] = xr_ref[CHUNK - SEGS:, :]
    hi_out[...] = xi_ref[CHUNK - SEGS:, :]
    o_ref[...] = _s5_tail(xr_ref[...].astype(BF16), xi_ref[...].astype(BF16), u, wc_ref, d_ref, wglu_ref)


def _s5_prompt(u_perm, wb, a, aseg, wc, d_flat, wglu_bf16, batch):
    nch = SEQ // CHUNK
    const = lambda shape: pl.BlockSpec(shape, lambda b, c: (0, 0))
    return pl.pallas_call(
        _s5_prompt_kernel,
        grid=(batch, nch),
        in_specs=[pl.BlockSpec((CHUNK, D_SSM), lambda b, c: (b * nch + c, 0)),
                  const((D_SSM, 2 * N_STATE)), const((2, N_STATE)), const((2, N_STATE)),
                  const((2 * N_STATE, D_SSM)), const((1, D_SSM)), const((D_SSM, 2 * D_SSM))],
        out_specs=[pl.BlockSpec((CHUNK, D_SSM), lambda b, c: (b * nch + c, 0)),
                   pl.BlockSpec((None, SEGS, N_STATE), lambda b, c: (b, 0, 0)),
                   pl.BlockSpec((None, SEGS, N_STATE), lambda b, c: (b, 0, 0))],
        out_shape=[jax.ShapeDtypeStruct((batch * SEQ, D_SSM), F32),
                   jax.ShapeDtypeStruct((batch, SEGS, N_STATE), F32),
                   jax.ShapeDtypeStruct((batch, SEGS, N_STATE), F32)],
        scratch_shapes=[pltpu.VMEM((CHUNK, N_STATE), F32), pltpu.VMEM((CHUNK, N_STATE), F32),
                        pltpu.VMEM((SEGS, N_STATE), F32), pltpu.VMEM((SEGS, N_STATE), F32),
                        pltpu.VMEM((SEGS, N_STATE), F32), pltpu.VMEM((SEGS, N_STATE), F32),
                        pltpu.VMEM((1, N_STATE), F32), pltpu.VMEM((1, N_STATE), F32)],
        compiler_params=_cparams(("arbitrary", "arbitrary")),
        name="s5_prompt",
    )(u_perm, wb, a, aseg, wc, d_flat, wglu_bf16)


def _s5_sample_kernel(u_ref, h0r_ref, h0i_ref, wb_ref, a_ref, wc_ref, d_ref, wglu_ref,
                      o_ref, hr_out, hi_out, xr_ref, xi_ref):
    nb = h0r_ref.shape[0]
    u = u_ref[...]
    u_hi = u.astype(BF16)
    u_lo = (u - u_hi.astype(F32)).astype(BF16)
    xr_ref[...] = (jnp.dot(u_hi, wb_ref[:, :N_STATE], preferred_element_type=F32)
                   + jnp.dot(u_lo, wb_ref[:, :N_STATE], preferred_element_type=F32))
    xi_ref[...] = (jnp.dot(u_hi, wb_ref[:, N_STATE:], preferred_element_type=F32)
                   + jnp.dot(u_lo, wb_ref[:, N_STATE:], preferred_element_type=F32))
    ar, ai = a_ref[0:1, :], a_ref[1:2, :]
    hr, hi = h0r_ref[...], h0i_ref[...]
    for t in range(DEC_SEQ):
        rows = slice(t * nb, (t + 1) * nb)
        hr, hi = ar * hr - ai * hi + xr_ref[rows, :], ar * hi + ai * hr + xi_ref[rows, :]
        xr_ref[rows, :] = hr
        xi_ref[rows, :] = hi
    hr_out[...] = hr
    hi_out[...] = hi
    o_ref[...] = _s5_tail(xr_ref[...].astype(BF16), xi_ref[...].astype(BF16), u, wc_ref, d_ref, wglu_ref)


def _s5_sample(u_tb, h0r, h0i, wb, a, wc, d_flat, wglu_bf16):
    rows = u_tb.shape[0]
    nb = h0r.shape[0]
    return pl.pallas_call(
        _s5_sample_kernel,
        out_shape=[jax.ShapeDtypeStruct((rows, D_SSM), F32),
                   jax.ShapeDtypeStruct((nb, N_STATE), F32),
                   jax.ShapeDtypeStruct((nb, N_STATE), F32)],
        scratch_shapes=[pltpu.VMEM((rows, N_STATE), F32), pltpu.VMEM((rows, N_STATE), F32)],
        compiler_params=pltpu.CompilerParams(vmem_limit_bytes=VMEM_LIMIT),
        name="s5_sample",
    )(u_tb, h0r, h0i, wb, a, wc, d_flat, wglu_bf16)


TQM = 512


def _mem_prompt_kernel(q_ref, k_ref, v_ref, o_ref):
    for h in range(N_MEM_HEADS):
        cs = slice(h * HEAD_DIM, (h + 1) * HEAD_DIM)
        s = lax.dot_general(q_ref[:, cs], k_ref[:, cs], NT, preferred_element_type=F32) * ATT_SCALE
        p = jnp.exp(s - jnp.max(s, axis=1, keepdims=True))
        l = jnp.sum(p, axis=1, keepdims=True)
        o_ref[:, cs] = jnp.dot(p.astype(BF16), v_ref[:, cs], preferred_element_type=F32) / l


def _mem_prompt(zb, kvb, batch):
    nqt = SEQ // TQM
    return pl.pallas_call(
        _mem_prompt_kernel,
        grid=(batch, nqt),
        in_specs=[pl.BlockSpec((TQM, D_MEM), lambda b, i: (b * nqt + i, C_QM // D_MEM)),
                  pl.BlockSpec((N_MEM, D_MEM), lambda b, i: (b, 0)),
                  pl.BlockSpec((N_MEM, D_MEM), lambda b, i: (b, 1))],
        out_specs=pl.BlockSpec((TQM, D_MEM), lambda b, i: (b * nqt + i, 0)),
        out_shape=jax.ShapeDtypeStruct((batch * SEQ, D_MEM), F32),
        compiler_params=_cparams(("arbitrary", "arbitrary")),
        name="mem_prompt",
    )(zb, kvb, kvb)


MEM_G = 8
MEM_TPAD = SUBLANES


def _mem_sample_kernel(q_ref, k_ref, v_ref, o_ref):
    for bb in range(MEM_G):
        for h in range(N_MEM_HEADS):
            rs = slice((bb * N_MEM_HEADS + h) * MEM_TPAD, (bb * N_MEM_HEADS + h + 1) * MEM_TPAD)
            kh = k_ref[bb, pl.ds(h, N_MEM, stride=N_MEM_HEADS), :].astype(BF16)
            vh = v_ref[bb, pl.ds(h, N_MEM, stride=N_MEM_HEADS), :].astype(BF16)
            s = lax.dot_general(q_ref[rs, :].astype(BF16), kh, NT, preferred_element_type=F32) * ATT_SCALE
            p = jnp.exp(s - jnp.max(s, axis=1, keepdims=True))
            l = jnp.sum(p, axis=1, keepdims=True)
            o_ref[rs, :] = jnp.dot(p.astype(BF16), vh, preferred_element_type=F32) / l


def _mem_sample(q_pad, mem_k, mem_v):
    nb = mem_k.shape[0]
    rows = MEM_G * N_MEM_HEADS * MEM_TPAD
    return pl.pallas_call(
        _mem_sample_kernel,
        grid=(nb // MEM_G,),
        in_specs=[pl.BlockSpec((rows, HEAD_DIM), lambda i: (i, 0)),
                  pl.BlockSpec((MEM_G, N_MEM * N_MEM_HEADS, HEAD_DIM), lambda i: (i, 0, 0)),
                  pl.BlockSpec((MEM_G, N_MEM * N_MEM_HEADS, HEAD_DIM), lambda i: (i, 0, 0))],
        out_specs=pl.BlockSpec((rows, HEAD_DIM), lambda i: (i, 0)),
        out_shape=jax.ShapeDtypeStruct((nb * N_MEM_HEADS * MEM_TPAD, HEAD_DIM), F32),
        compiler_params=_cparams(("arbitrary",)),
        name="mem_sample",
    )(q_pad, mem_k, mem_v)


def _out_kernel(x_ref, oa_ref, os_ref, om_ref, ga_ref, gs_ref, gm_ref, w_ref, g_ref, y_ref):
    def gated(o_ref, gate_ref):
        gate = gate_ref[...]
        return (o_ref[...] * (gate * _sigmoid(gate))).astype(BF16)

    y = (x_ref[...]
         + jnp.dot(gated(oa_ref, ga_ref), w_ref[:D_ATT, :], preferred_element_type=F32)
         + jnp.dot(gated(os_ref, gs_ref), w_ref[D_ATT:D_ATT + D_SSM, :], preferred_element_type=F32)
         + jnp.dot(gated(om_ref, gm_ref), w_ref[D_ATT + D_SSM:, :], preferred_element_type=F32))
    ms = jnp.mean(y * y, axis=-1, keepdims=True)
    y_ref[...] = y * lax.rsqrt(ms + EPS) * g_ref[...]


def _out_proj(x, o_a, o_s, o_m, z, w_out_bf16, g_final, tm):
    m = x.shape[0]
    rows = lambda w, c: pl.BlockSpec((tm, w), lambda i: (i, c))
    return pl.pallas_call(
        _out_kernel,
        grid=(m // tm,),
        in_specs=[rows(D_MODEL, 0), rows(D_ATT, 0), rows(D_SSM, 0), rows(D_MEM, 0),
                  rows(D_ATT, C_GA // D_ATT), rows(D_SSM, C_GS // D_SSM), rows(D_MEM, C_GM // D_MEM),
                  pl.BlockSpec((D_MODEL, D_MODEL), lambda i: (0, 0)),
                  pl.BlockSpec((1, D_MODEL), lambda i: (0, 0))],
        out_specs=rows(D_MODEL, 0),
        out_shape=jax.ShapeDtypeStruct((m, D_MODEL), F32),
        compiler_params=_cparams(("arbitrary",)),
        name="out_proj",
    )(x, o_a, o_s, o_m, z, z, z, w_out_bf16, g_final.reshape(1, D_MODEL))


def _reorder_w_in(w):
    q, k, v, ga, qi, ki, wi, u, gs, qm, gm = jnp.split(
        w, [1024, 1280, 1536, 2560, 3584, 3648, 3664, 4176, 4688, 5200], axis=1)
    pad = jnp.zeros((w.shape[0], D_INP - w.shape[1]), w.dtype)
    return jnp.concatenate([q, qi, ga, k, v, u, gs, qm, gm, ki, wi, pad], axis=1).astype(BF16)


def kernel(x_prompt, x_sample, mem_prompt, cache_k, cache_v, cache_kidx, cache_mem_k, cache_mem_v,
           state_ssm_re, state_ssm_im, page_table, g_norm, w_in, g_mem, w_mem_kv, lam_re, lam_im,
           log_dt, b_re, b_im, c_re, c_im, d_skip, w_glu, w_out, g_final):
    depth = w_in.shape[0]
    assert depth == 1
    l = 0
    bp, db = x_prompt.shape[0], x_sample.shape[0]
    tp, ts = bp * SEQ, db * DEC_SEQ

    w_in_b = _reorder_w_in(w_in[l])
    wb, wc, a, aseg = _ssm_params(lam_re[l], lam_im[l], log_dt[l], b_re[l], b_im[l], c_re[l], c_im[l])
    d_flat = d_skip[l].reshape(1, D_SSM)
    wglu_b = w_glu[l].astype(BF16)
    w_out_b = w_out[l].astype(BF16)

    xp = x_prompt.reshape(tp, D_MODEL)
    xs = x_sample.reshape(ts, D_MODEL)
    zp, zpb = _proj(xp, g_norm[l], w_in_b, 1024, 640)
    zs, zsb = _proj(xs, g_norm[l], w_in_b, 512, 640)
    kv, kvb = _proj(mem_prompt.reshape(bp * N_MEM, D_MODEL), g_mem[l], w_mem_kv[l].astype(BF16), 512, 512)

    oa_p = _dsa_prompt(zp, zpb, bp)
    nch = SEQ // CHUNK
    u_p = zp[:, C_U:C_U + D_SSM].reshape(bp, nch, SEGS, SEG_LEN, D_SSM).transpose(0, 1, 3, 2, 4).reshape(tp, D_SSM)
    os_perm, hr_p, hi_p = _s5_prompt(u_p, wb, a, aseg, wc, d_flat, wglu_b, bp)
    os_p = os_perm.reshape(bp, nch, SEG_LEN, SEGS, D_SSM).transpose(0, 1, 3, 2, 4).reshape(tp, D_SSM)
    om_p = _mem_prompt(zpb, kvb, bp)
    y_p = _out_proj(xp, oa_p, os_p, om_p, zp, w_out_b, g_final, 256)

    pool = cache_k.shape[1]
    q_rows = zsb[:, C_Q:C_Q + D_ATT].reshape(ts * N_HEADS, HEAD_DIM)
    qis_rows = zsb[:, C_QI:C_QI + N_IDX * IDX_DIM].reshape(ts * N_IDX, IDX_DIM)
    wib_rows = jnp.broadcast_to(zs[:, C_KW + IDX_DIM:C_KW + IDX_DIM + N_IDX].reshape(ts * N_IDX, 1),
                                (ts * N_IDX, LANES))

    def new_page(cols, rows_per_token, width):
        rows = zs[:, cols:cols + rows_per_token * width].reshape(db, DEC_SEQ * rows_per_token, width)
        return jnp.pad(rows, ((0, 0), (0, (PAGE - DEC_SEQ) * rows_per_token), (0, 0)))

    oa_rows = _dsa_sample(page_table, q_rows, qis_rows, wib_rows,
                          cache_kidx[l],
                          cache_k[l].reshape(pool, N_KV * PAGE, HEAD_DIM),
                          cache_v[l].reshape(pool, N_KV * PAGE, HEAD_DIM),
                          new_page(C_KW, 1, IDX_DIM), new_page(C_K, N_KV, HEAD_DIM), new_page(C_V, N_KV, HEAD_DIM))
    oa_s = oa_rows.reshape(ts, D_ATT)

    u_tb = zs[:, C_U:C_U + D_SSM].reshape(db, DEC_SEQ, D_SSM).transpose(1, 0, 2).reshape(ts, D_SSM)
    os_tb, hr_s, hi_s = _s5_sample(u_tb, state_ssm_re[l].reshape(db, N_STATE), state_ssm_im[l].reshape(db, N_STATE),
                                   wb, a, wc, d_flat, wglu_b)
    os_s = os_tb.reshape(DEC_SEQ, db, D_SSM).transpose(1, 0, 2).reshape(ts, D_SSM)

    qm = zs[:, C_QM:C_QM + D_MEM].reshape(db, DEC_SEQ, N_MEM_HEADS, HEAD_DIM).transpose(0, 2, 1, 3)
    q_pad = jnp.pad(qm, ((0, 0), (0, 0), (0, MEM_TPAD - DEC_SEQ), (0, 0))).reshape(db * N_MEM_HEADS * MEM_TPAD, HEAD_DIM)
    om_pad = _mem_sample(q_pad, cache_mem_k[l].reshape(db, N_MEM * N_MEM_HEADS, HEAD_DIM),
                         cache_mem_v[l].reshape(db, N_MEM * N_MEM_HEADS, HEAD_DIM))
    om_s = om_pad.reshape(db, N_MEM_HEADS, MEM_TPAD, HEAD_DIM)[:, :, :DEC_SEQ].transpose(0, 2, 1, 3).reshape(ts, D_MEM)
    y_s = _out_proj(xs, oa_s, os_s, om_s, zs, w_out_b, g_final, 256)

    kvshape = (depth, bp, SEQ, N_KV, HEAD_DIM)
    skv = (depth, db, DEC_SEQ, N_KV, HEAD_DIM)
    width = N_KV * HEAD_DIM
    return (y_p.reshape(bp, SEQ, D_MODEL), y_s.reshape(db, DEC_SEQ, D_MODEL),
            zp[:, C_K:C_K + width].reshape(kvshape), zp[:, C_V:C_V + width].reshape(kvshape),
            zp[:, C_KW:C_KW + IDX_DIM].reshape(depth, bp, SEQ, IDX_DIM),
            kv[:, :D_MEM].reshape(depth, bp, N_MEM, N_MEM_HEADS, HEAD_DIM),
            kv[:, D_MEM:].reshape(depth, bp, N_MEM, N_MEM_HEADS, HEAD_DIM),
            hr_p[:, SEGS - 1].reshape(depth, bp, N_GROUPS, SSM_STATE),
            hi_p[:, SEGS - 1].reshape(depth, bp, N_GROUPS, SSM_STATE),
            zs[:, C_K:C_K + width].reshape(skv), zs[:, C_V:C_V + width].reshape(skv),
            zs[:, C_KW:C_KW + IDX_DIM].reshape(depth, db, DEC_SEQ, IDX_DIM),
            hr_s.reshape(depth, db, N_GROUPS, SSM_STATE), hi_s.reshape(depth, db, N_GROUPS, SSM_STATE))
```
